```python
import math
import jax
import jax.numpy as jnp
from jax import lax
import numpy as np

D_MODEL = 1024
BATCH = 2
SEQ = 8192
DEPTH = 1
DEC_BATCH = 32
DEC_SEQ = 4
PAST_LEN = 16384
PAGE_SIZE = 128

HEAD_DIM = 64
ATTN_WIDTH = D_MODEL // 2
N_HEADS = ATTN_WIDTH // HEAD_DIM
GQA = 4
N_KV = N_HEADS // GQA
N_BRANCH = 3
KV_COLS = 2 * N_KV * HEAD_DIM
SSM_WIDTH = D_MODEL - ATTN_WIDTH
SSM_GROUP = 16
N_SSM_GROUPS = SSM_WIDTH // SSM_GROUP
SSM_STATE = 64
IN_COLS = ATTN_WIDTH + N_BRANCH * KV_COLS + N_BRANCH * N_HEADS + SSM_WIDTH
CMP_LEN = 32
CMP_STRIDE = 16
CMP_HID = 2 * HEAD_DIM
SLC_BLOCK = 64
TOP_K = 16
WINDOW = 512
Q_CHUNK = 128
ROT_DIM = HEAD_DIM // 4
ROPE_THETA = 500000.0
D_FF = 256 * math.ceil(8 * D_MODEL / 3 / 256)
EPS = 1e-6
NEG = -1e30
BIG = 1e4

kernel_name = "hymba_nsa_s5_adaln_decode_step"


def rmsnorm(x, g):
    xf = x.astype(jnp.float32)
    y = xf * lax.rsqrt(jnp.mean(xf * xf, axis=-1, keepdims=True) + EPS)
    return (y * g.astype(jnp.float32)).astype(x.dtype)


def rope(x, pos):
    half = ROT_DIM // 2
    inv = ROPE_THETA ** (-(jnp.arange(half, dtype=jnp.float32) * 2.0 / ROT_DIM))
    ang = pos.astype(jnp.float32)[:, None] * inv[None, :]
    shape = (1, pos.shape[0]) + (1,) * (x.ndim - 3) + (half,)
    cos = jnp.cos(ang).reshape(shape)
    sin = jnp.sin(ang).reshape(shape)
    xf = x.astype(jnp.float32)
    x1 = xf[..., :half]
    x2 = xf[..., half:ROT_DIM]
    out = jnp.concatenate([x1 * cos - x2 * sin, x1 * sin + x2 * cos, xf[..., ROT_DIM:]], axis=-1)
    return out.astype(x.dtype)


def masked_softmax(s, mask):
    p = jax.nn.softmax(jnp.where(mask, s, NEG), axis=-1)
    return jnp.where(mask, p, 0.0)


def block_overlap(nc, ns):
    cs = np.arange(nc) * CMP_STRIDE
    ss = np.arange(ns) * SLC_BLOCK
    ov = np.minimum(cs[:, None] + CMP_LEN, ss[None, :] + SLC_BLOCK) - np.maximum(cs[:, None], ss[None, :])
    return jnp.asarray((np.clip(ov, 0, None) / CMP_LEN).astype(np.float32))


def compress(rows, pe, w1, w2):
    bn, length, g, dh = rows.shape
    r = CMP_LEN // CMP_STRIDE
    n_chunk = length // CMP_STRIDE
    nc = n_chunk - r + 1
    ch = rows[:, :n_chunk * CMP_STRIDE].reshape(bn, n_chunk, CMP_STRIDE, g, dh)
    blk = jnp.concatenate([ch[:, m:m + nc] for m in range(r)], axis=2) + pe[None, None, :, None, :]
    flat = blk.transpose(0, 1, 3, 2, 4).reshape(bn, nc, g, CMP_LEN * dh)
    return jax.nn.gelu(flat @ w1) @ w2


def to_blocks(rows):
    bn, length, g, dh = rows.shape
    ns = -(-length // SLC_BLOCK)
    r = jnp.pad(rows, ((0, 0), (0, ns * SLC_BLOCK - length), (0, 0), (0, 0)))
    return r.reshape(bn, ns, SLC_BLOCK, g, dh).transpose(0, 3, 1, 2, 4)


def nsa_attend(q, q_pos, k_cmp, v_cmp, cmp_end, k_blk, v_blk, kv_win, win_pos, gates):
    bn, nq = q.shape[:2]
    nc = k_cmp.shape[1]
    ns = k_blk.shape[2]
    scale = HEAD_DIM ** -0.5
    qg = q.reshape(bn, nq, N_KV, GQA, HEAD_DIM)
    s_c = jnp.einsum('bqgrd,bcgd->bqgrc', qg, k_cmp).astype(jnp.float32) * scale
    m_c = (cmp_end[None, :] <= q_pos[:, None])[None, :, None, None, :]
    p_c = masked_softmax(s_c, m_c)
    o_c = jnp.einsum('bqgrc,bcgd->bqgrd', p_c.astype(v_cmp.dtype), v_cmp)
    imp = jnp.einsum('bqgrc,cj->bqgj', p_c, block_overlap(nc, ns))
    cur = q_pos // SLC_BLOCK
    j = jnp.arange(ns)
    future = (j[None, :] > cur[:, None])[None, :, None, :]
    forced = ((j[None, :] == 0) | (j[None, :] == cur[:, None]) | (j[None, :] == cur[:, None] - 1))[None, :, None, :]
    imp = jnp.where(future, NEG, jnp.where(forced, BIG, imp))
    _, idx = lax.top_k(imp, min(TOP_K, ns))
    bi = jnp.arange(bn)[:, None, None, None]
    gi = jnp.arange(N_KV)[None, None, :, None]
    k_sel = k_blk[bi, gi, idx]
    v_sel = v_blk[bi, gi, idx]
    n_sel = idx.shape[-1] * SLC_BLOCK
    key_pos = (idx[..., None] * SLC_BLOCK + jnp.arange(SLC_BLOCK)).reshape(bn, nq, N_KV, 1, n_sel)
    s_s = jnp.einsum('bqgrd,bqgkjd->bqgrkj', qg, k_sel).reshape(bn, nq, N_KV, GQA, n_sel)
    p_s = masked_softmax(s_s.astype(jnp.float32) * scale, key_pos <= q_pos[None, :, None, None, None])
    o_s = jnp.einsum('bqgrn,bqgnd->bqgrd', p_s.astype(v_sel.dtype), v_sel.reshape(bn, nq, N_KV, n_sel, HEAD_DIM))
    s_w = jnp.einsum('bqgrd,bwgd->bqgrw', qg, kv_win[:, :, 0]).astype(jnp.float32) * scale
    m_w = (win_pos[None, :] <= q_pos[:, None]) & (win_pos[None, :] >= q_pos[:, None] - WINDOW) & (win_pos[None, :] >= 0)
    p_w = masked_softmax(s_w, m_w[None, :, None, None, :])
    o_w = jnp.einsum('bqgrw,bwgd->bqgrd', p_w.astype(kv_win.dtype), kv_win[:, :, 1])
    g = gates.reshape(bn, nq, N_KV, GQA, N_BRANCH)
    o = g[..., 0:1] * o_c + g[..., 1:2] * o_s + g[..., 2:3] * o_w
    return o.reshape(bn, nq, ATTN_WIDTH)


def complex_affine_combine(e1, e2):
    a1r, a1i, b1r, b1i = e1
    a2r, a2i, b2r, b2i = e2
    return (a2r * a1r - a2i * a1i, a2r * a1i + a2i * a1r,
            a2r * b1r - a2i * b1i + b2r, a2r * b1i + a2i * b1r + b2i)


def s5_scan(u, h0_re, h0_im, prm):
    f32 = jnp.float32
    bn, length, _ = u.shape
    uf = u.astype(f32).reshape(bn, length, N_SSM_GROUPS, SSM_GROUP)
    a_re = prm['ssm_a_re'].astype(f32)
    a_im = prm['ssm_a_im'].astype(f32)
    dt = jnp.exp(prm['ssm_log_dt'].astype(f32))[:, None]
    mag = jnp.exp(a_re * dt)
    lb_re = mag * jnp.cos(a_im * dt)
    lb_im = mag * jnp.sin(a_im * dt)
    den = a_re * a_re + a_im * a_im
    n_re = lb_re - 1.0
    f_re = (n_re * a_re + lb_im * a_im) / den
    f_im = (lb_im * a_re - n_re * a_im) / den
    b_re = prm['ssm_b_re'].astype(f32)
    b_im = prm['ssm_b_im'].astype(f32)
    bb_re = f_re[..., None] * b_re - f_im[..., None] * b_im
    bb_im = f_re[..., None] * b_im + f_im[..., None] * b_re
    bu_re = jnp.einsum('blgi,gpi->blgp', uf, bb_re)
    bu_im = jnp.einsum('blgi,gpi->blgp', uf, bb_im)
    h0r = h0_re.astype(f32)
    h0i = h0_im.astype(f32)
    bu_re = bu_re.at[:, 0].add(lb_re * h0r - lb_im * h0i)
    bu_im = bu_im.at[:, 0].add(lb_re * h0i + lb_im * h0r)
    ar = jnp.broadcast_to(lb_re, bu_re.shape)
    ai = jnp.broadcast_to(lb_im, bu_im.shape)
    _, _, h_re, h_im = lax.associative_scan(complex_affine_combine, (ar, ai, bu_re, bu_im), axis=1)
    y = (jnp.einsum('blgp,gip->blgi', h_re, prm['ssm_c_re'].astype(f32))
         - jnp.einsum('blgp,gip->blgi', h_im, prm['ssm_c_im'].astype(f32))
         + prm['ssm_d'].astype(f32) * uf)
    return y.reshape(bn, length, SSM_WIDTH).astype(u.dtype), h_re[:, -1], h_im[:, -1]


def project(x, c, pos, prm):
    bn, length, _ = x.shape
    ada = (jax.nn.silu(c) @ prm['w_ada'] + prm['b_ada']).reshape(bn, 6, D_MODEL)
    shift1, scale1, gate1, shift2, scale2, gate2 = [ada[:, i][:, None, :] for i in range(6)]
    h = rmsnorm(x, prm['norm_mix_g']) * (1.0 + scale1) + shift1
    z = h @ prm['w_in']
    o1 = ATTN_WIDTH
    o2 = o1 + N_BRANCH * KV_COLS
    o3 = o2 + N_BRANCH * N_HEADS
    q = rope(rmsnorm(z[..., :o1].reshape(bn, length, N_HEADS, HEAD_DIM), prm['q_norm_g']), pos)
    kv = z[..., o1:o2].reshape(bn, length, N_BRANCH, 2, N_KV, HEAD_DIM)
    k = rope(rmsnorm(kv[:, :, :, 0], prm['k_norm_g'][:, None, :]), pos)
    kv = jnp.stack([k, kv[:, :, :, 1]], axis=3)
    gates = jax.nn.sigmoid(z[..., o2:o3]).reshape(bn, length, N_HEADS, N_BRANCH)
    u = z[..., o3:]
    return q, kv[:, :, 0], kv[:, :, 1], kv[:, :, 2], gates, u, (gate1, shift2, scale2, gate2)


def finish(x, attn, y_ssm, ada, prm):
    gate1, shift2, scale2, gate2 = ada
    s = jax.nn.gelu(y_ssm)
    s = s * jax.nn.sigmoid(s @ prm['w_glu'] + prm['b_glu'])
    mix = jnp.concatenate([rmsnorm(attn, prm['attn_out_g']), rmsnorm(s, prm['ssm_out_g'])], axis=-1) @ prm['w_out']
    x = x + gate1 * mix
    h = rmsnorm(x, prm['norm_ffn_g']) * (1.0 + scale2) + shift2
    f = (jax.nn.silu(h @ prm['w_ffn_gate']) * (h @ prm['w_ffn_up'])) @ prm['w_ffn_down']
    return x + gate2 * f


def compressed_kv(rows, prm):
    k_cmp = compress(rows[:, :, 0], prm['cmp_pe_k'], prm['cmp_w1_k'], prm['cmp_w2_k'])
    v_cmp = compress(rows[:, :, 1], prm['cmp_pe_v'], prm['cmp_w1_v'], prm['cmp_w2_v'])
    cmp_end = jnp.arange(k_cmp.shape[1]) * CMP_STRIDE + CMP_LEN - 1
    return k_cmp, v_cmp, cmp_end


def prompt_layer(x, c, prm):
    bn, length, _ = x.shape
    pos = jnp.arange(length)
    q, kv_c, kv_s, kv_w, gates, u, ada = project(x, c, pos, prm)
    k_cmp, v_cmp, cmp_end = compressed_kv(kv_c, prm)
    k_blk = to_blocks(kv_s[:, :, 0])
    v_blk = to_blocks(kv_s[:, :, 1])
    kv_w_pad = jnp.pad(kv_w, ((0, 0), (WINDOW, 0), (0, 0), (0, 0), (0, 0)))

    def chunk(ci):
        start = ci * Q_CHUNK
        q_c = lax.dynamic_slice_in_dim(q, start, Q_CHUNK, axis=1)
        g_c = lax.dynamic_slice_in_dim(gates, start, Q_CHUNK, axis=1)
        w_c = lax.dynamic_slice_in_dim(kv_w_pad, start, WINDOW + Q_CHUNK, axis=1)
        q_pos = start + jnp.arange(Q_CHUNK)
        w_pos = start - WINDOW + jnp.arange(WINDOW + Q_CHUNK)
        return nsa_attend(q_c, q_pos, k_cmp, v_cmp, cmp_end, k_blk, v_blk, w_c, w_pos, g_c)

    attn = lax.map(chunk, jnp.arange(length // Q_CHUNK))
    attn = jnp.moveaxis(attn, 0, 1).reshape(bn, length, ATTN_WIDTH)
    h0 = jnp.zeros((bn, N_SSM_GROUPS, SSM_STATE), jnp.float32)
    y_ssm, h_re, h_im = s5_scan(u, h0, h0, prm)
    y = finish(x, attn, y_ssm, ada, prm)
    keep = min(WINDOW, length)
    return y, (kv_c, kv_s, kv_w[:, length - keep:], h_re, h_im)


def sample_layer(x, c, l, cache_kv_cmp, cache_kv_slc, cache_kv_win, state_ssm_re, state_ssm_im, page_table, prm):
    bn, length, _ = x.shape
    past = page_table.shape[1] * PAGE_SIZE
    pos = past + jnp.arange(length)
    q, kv_c, kv_s, kv_w, gates, u, ada = project(x, c, pos, prm)
    row_shape = (bn, past, 2, N_KV, HEAD_DIM)
    full_c = jnp.concatenate([cache_kv_cmp[l, page_table].reshape(row_shape).astype(kv_c.dtype), kv_c], axis=1)
    full_s = jnp.concatenate([cache_kv_slc[l, page_table].reshape(row_shape).astype(kv_s.dtype), kv_s], axis=1)
    k_cmp, v_cmp, cmp_end = compressed_kv(full_c, prm)
    k_blk = to_blocks(full_s[:, :, 0])
    v_blk = to_blocks(full_s[:, :, 1])
    win = jnp.concatenate([cache_kv_win[l].astype(kv_w.dtype), kv_w], axis=1)
    w_pos = past - cache_kv_win.shape[2] + jnp.arange(win.shape[1])
    attn = nsa_attend(q, pos, k_cmp, v_cmp, cmp_end, k_blk, v_blk, win, w_pos, gates)
    y_ssm, h_re, h_im = s5_scan(u, state_ssm_re[l], state_ssm_im[l], prm)
    y = finish(x, attn, y_ssm, ada, prm)
    return y, (kv_c, kv_s, kv_w, h_re, h_im)


def setup_inputs(seed: int = 0) -> dict:
    key = jax.random.key(seed)
    ks = iter(jax.random.split(key, 48))
    f32 = jnp.float32

    def nrm(shape, s):
        return jax.random.normal(next(ks), shape, f32) * s

    n_pages = PAST_LEN // PAGE_SIZE
    n_used = DEC_BATCH * n_pages
    n_phys = n_used + max(1, n_used // 4)
    win_buf = min(WINDOW, PAST_LEN)
    dp = DEPTH
    g, p = N_SSM_GROUPS, SSM_STATE
    x_prompt = nrm((BATCH, SEQ, D_MODEL), 1.0)
    x_sample = nrm((DEC_BATCH, DEC_SEQ, D_MODEL), 1.0)
    cache_kv_cmp = nrm((dp, n_phys, PAGE_SIZE, 2, N_KV, HEAD_DIM), 1.0)
    cache_kv_slc = nrm((dp, n_phys, PAGE_SIZE, 2, N_KV, HEAD_DIM), 1.0)
    cache_kv_win = nrm((dp, DEC_BATCH, win_buf, 2, N_KV, HEAD_DIM), 1.0)
    state_ssm_re = nrm((dp, DEC_BATCH, g, p), 1.0)
    state_ssm_im = nrm((dp, DEC_BATCH, g, p), 1.0)
    page_table = jax.random.permutation(next(ks), n_phys)[:n_used].reshape(DEC_BATCH, n_pages).astype(jnp.int32)
    c_prompt = nrm((BATCH, D_MODEL), 1.0)
    c_sample = nrm((DEC_BATCH, D_MODEL), 1.0)
    return {
        'x_prompt': x_prompt, 'x_sample': x_sample,
        'cache_kv_cmp': cache_kv_cmp, 'cache_kv_slc': cache_kv_slc, 'cache_kv_win': cache_kv_win,
        'state_ssm_re': state_ssm_re, 'state_ssm_im': state_ssm_im, 'page_table': page_table,
        'c_prompt': c_prompt, 'c_sample': c_sample,
        'norm_mix_g': 1.0 + nrm((dp, D_MODEL), 0.01),
        'w_ada': nrm((dp, D_MODEL, 6 * D_MODEL), 0.5 * D_MODEL ** -0.5),
        'b_ada': nrm((dp, 6 * D_MODEL), 0.01),
        'w_in': nrm((dp, D_MODEL, IN_COLS), D_MODEL ** -0.5),
        'q_norm_g': 1.0 + nrm((dp, HEAD_DIM), 0.01),
        'k_norm_g': 1.0 + nrm((dp, N_BRANCH, HEAD_DIM), 0.01),
        'cmp_pe_k': nrm((dp, CMP_LEN, HEAD_DIM), 0.1),
        'cmp_w1_k': nrm((dp, CMP_LEN * HEAD_DIM, CMP_HID), (CMP_LEN * HEAD_DIM) ** -0.5),
        'cmp_w2_k': nrm((dp, CMP_HID, HEAD_DIM), CMP_HID ** -0.5),
        'cmp_pe_v': nrm((dp, CMP_LEN, HEAD_DIM), 0.1),
        'cmp_w1_v': nrm((dp, CMP_LEN * HEAD_DIM, CMP_HID), (CMP_LEN * HEAD_DIM) ** -0.5),
        'cmp_w2_v': nrm((dp, CMP_HID, HEAD_DIM), CMP_HID ** -0.5),
        'ssm_a_re': -0.5 + nrm((dp, g, p), 0.01),
        'ssm_a_im': jnp.pi * jnp.arange(p, dtype=f32)[None, None, :] + nrm((dp, g, p), 0.01),
        'ssm_log_dt': jax.random.uniform(next(ks), (dp, g), f32, math.log(1e-3), math.log(1e-1)),
        'ssm_b_re': nrm((dp, g, p, SSM_GROUP), (2 * SSM_GROUP) ** -0.5),
        'ssm_b_im': nrm((dp, g, p, SSM_GROUP), (2 * SSM_GROUP) ** -0.5),
        'ssm_c_re': nrm((dp, g, SSM_GROUP, p), (2 * p) ** -0.5),
        'ssm_c_im': nrm((dp, g, SSM_GROUP, p), (2 * p) ** -0.5),
        'ssm_d': nrm((dp, g, SSM_GROUP), 1.0),
        'w_glu': nrm((dp, SSM_WIDTH, SSM_WIDTH), SSM_WIDTH ** -0.5),
        'b_glu': nrm((dp, SSM_WIDTH), 0.01),
        'attn_out_g': 1.0 + nrm((dp, ATTN_WIDTH), 0.01),
        'ssm_out_g': 1.0 + nrm((dp, SSM_WIDTH), 0.01),
        'w_out': nrm((dp, ATTN_WIDTH + SSM_WIDTH, D_MODEL), (ATTN_WIDTH + SSM_WIDTH) ** -0.5),
        'norm_ffn_g': 1.0 + nrm((dp, D_MODEL), 0.01),
        'w_ffn_gate': nrm((dp, D_MODEL, D_FF), D_MODEL ** -0.5),
        'w_ffn_up': nrm((dp, D_MODEL, D_FF), D_MODEL ** -0.5),
        'w_ffn_down': nrm((dp, D_FF, D_MODEL), D_FF ** -0.5),
    }


def reference(x_prompt, x_sample, cache_kv_cmp, cache_kv_slc, cache_kv_win, state_ssm_re, state_ssm_im,
              page_table, c_prompt, c_sample, norm_mix_g, w_ada, b_ada, w_in, q_norm_g, k_norm_g,
              cmp_pe_k, cmp_w1_k, cmp_w2_k, cmp_pe_v, cmp_w1_v, cmp_w2_v, ssm_a_re, ssm_a_im, ssm_log_dt,
              ssm_b_re, ssm_b_im, ssm_c_re, ssm_c_im, ssm_d, w_glu, b_glu, attn_out_g, ssm_out_g, w_out,
              norm_ffn_g, w_ffn_gate, w_ffn_up, w_ffn_down):
    xp, xs = x_prompt, x_sample
    outs = [[] for _ in range(10)]
    for l in range(DEPTH):
        prm = dict(
            norm_mix_g=norm_mix_g[l], w_ada=w_ada[l], b_ada=b_ada[l], w_in=w_in[l],
            q_norm_g=q_norm_g[l], k_norm_g=k_norm_g[l],
            cmp_pe_k=cmp_pe_k[l], cmp_w1_k=cmp_w1_k[l], cmp_w2_k=cmp_w2_k[l],
            cmp_pe_v=cmp_pe_v[l], cmp_w1_v=cmp_w1_v[l], cmp_w2_v=cmp_w2_v[l],
            ssm_a_re=ssm_a_re[l], ssm_a_im=ssm_a_im[l], ssm_log_dt=ssm_log_dt[l],
            ssm_b_re=ssm_b_re[l], ssm_b_im=ssm_b_im[l], ssm_c_re=ssm_c_re[l], ssm_c_im=ssm_c_im[l],
            ssm_d=ssm_d[l], w_glu=w_glu[l], b_glu=b_glu[l], attn_out_g=attn_out_g[l],
            ssm_out_g=ssm_out_g[l], w_out=w_out[l], norm_ffn_g=norm_ffn_g[l],
            w_ffn_gate=w_ffn_gate[l], w_ffn_up=w_ffn_up[l], w_ffn_down=w_ffn_down[l])
        xp, st_p = prompt_layer(xp, c_prompt, prm)
        xs, st_s = sample_layer(xs, c_sample, l, cache_kv_cmp, cache_kv_slc, cache_kv_win,
                                state_ssm_re, state_ssm_im, page_table, prm)
        for i, a in enumerate(st_p + st_s):
            outs[i].append(a)
    n = [jnp.stack(o) for o in outs]
    return (xp, xs, n[0], n[1], n[2], n[3], n[4], n[5], n[6], n[7], n[8], n[9])
```

```python
import functools
import math

import numpy as np
import jax
import jax.numpy as jnp
from jax import lax
from jax.experimental import pallas as pl
from jax.experimental.pallas import tpu as pltpu

F32 = jnp.float32
BF16 = jnp.bfloat16
HIGHEST = lax.Precision.HIGHEST

D_MODEL = 1024
HEAD_DIM = 64
ATTN_WIDTH = 512
N_HEADS = 8
GQA = 4
N_KV = 2
N_BRANCH = 3
KV_COLS = 2 * N_KV * HEAD_DIM
SSM_WIDTH = 512
SSM_GROUP = 16
N_SSM_GROUPS = 32
SSM_STATE = 64
N_STATE = N_SSM_GROUPS * SSM_STATE
CMP_LEN = 32
CMP_STRIDE = 16
CMP_HID = 128
SLC_BLOCK = 64
TOP_K = 16
WINDOW = 512
Q_CHUNK = 128
ROT_DIM = 16
ROPE_THETA = 500000.0
D_FF = 2816
EPS = 1e-6
NEG = -1e30
BIG = 1e4
PAGE = 128

LANES = 128
SUBLANES = 8
VMEM_LIMIT = 56 * 1024 * 1024

COL_Q = 0
COL_KV = 512
COL_U = 1280
COL_GATE = 1792
IN_COLS_PACKED = 1920

SCAN_TILE = 1024
SCAN_SEG = 128
KV_TILE = 512
SLC_PAGES = 16


def _cparams(*sem):
    return pltpu.CompilerParams(dimension_semantics=sem, vmem_limit_bytes=VMEM_LIMIT)


def _split(a):
    hi = a.astype(BF16)
    lo = (a - hi.astype(F32)).astype(BF16)
    return hi, lo


def _dot(a, b):
    return jnp.dot(a, b, preferred_element_type=F32)


def _dot_t(a, b):
    return lax.dot_general(a, b, (((1,), (1,)), ((), ())), preferred_element_type=F32)


def _dotx(a, b):
    return jnp.dot(a, b, preferred_element_type=F32, precision=HIGHEST)


def _dotx_t(a, b):
    return lax.dot_general(a, b, (((1,), (1,)), ((), ())), preferred_element_type=F32, precision=HIGHEST)


def _mm3(a, b_hi, b_lo):
    a_hi, a_lo = _split(a)
    return _dot(a_hi, b_hi) + _dot(a_hi, b_lo) + _dot(a_lo, b_hi)


def _mm3_t(a_hi, a_lo, b):
    b_hi, b_lo = _split(b)
    return _dot_t(a_hi, b_hi) + _dot_t(a_hi, b_lo) + _dot_t(a_lo, b_hi)


def _vmem_full():
    return pl.BlockSpec(memory_space=pltpu.VMEM)


def _ada_kernel(c_ref, w_ref, b_ref, o_ref):
    c = c_ref[...]
    s = c * jax.nn.sigmoid(c)
    o_ref[...] = _dotx(s, w_ref[...]) + b_ref[...]


def _ada(c, w, b):
    rows = c.shape[0]
    ncol = w.shape[1]
    tn = 1024
    return pl.pallas_call(
        _ada_kernel,
        grid=(ncol // tn,),
        in_specs=[pl.BlockSpec((rows, D_MODEL), lambda j: (0, 0)),
                  pl.BlockSpec((D_MODEL, tn), lambda j: (0, j)),
                  pl.BlockSpec((1, tn), lambda j: (0, j))],
        out_specs=pl.BlockSpec((rows, tn), lambda j: (0, j)),
        out_shape=jax.ShapeDtypeStruct((rows, ncol), F32),
        compiler_params=_cparams("arbitrary"),
        name="ada",
    )(c, w, b)


def _proj_kernel(x_ref, mod_ref, g_ref, whi_ref, wlo_ref, qg_ref, kg_ref, bd_ref, c_ref, sp_ref, sm_ref,
                 q_ref, kvc_ref, kvs_ref, kvw_ref, kvsg_ref, kvwg_ref, gates_ref, u_ref):
    x = x_ref[...]
    ms = jnp.mean(x * x, axis=-1, keepdims=True)
    h = x * lax.rsqrt(ms + EPS) * g_ref[...]
    h = h * (1.0 + mod_ref[0, 1]) + mod_ref[0, 0]
    z = _mm3(h, whi_ref[...], wlo_ref[...])
    cos = c_ref[...]
    sin_p = sp_ref[...]
    sin_m = sm_ref[...]
    bd = bd_ref[...]
    lane = lax.broadcasted_iota(jnp.int32, (x.shape[0], LANES), 1)
    low = lane < HEAD_DIM

    def norm_rope(v, gvec):
        msq = _dotx(v * v, bd)
        y = v * lax.rsqrt(msq + EPS) * gvec
        return y * cos + pltpu.roll(y, ROT_DIM // 2, 1) * sin_p + pltpu.roll(y, LANES - ROT_DIM // 2, 1) * sin_m

    qg = qg_ref[...]
    for j in range(ATTN_WIDTH // LANES):
        q_ref[:, j * LANES:(j + 1) * LANES] = norm_rope(z[:, COL_Q + j * LANES:COL_Q + (j + 1) * LANES], qg) * (HEAD_DIM ** -0.5)
    kv_refs = (kvc_ref, kvs_ref, kvw_ref)
    g_refs = (None, kvsg_ref, kvwg_ref)
    for br in range(N_BRANCH):
        base = COL_KV + br * KV_COLS
        k = norm_rope(z[:, base:base + LANES], kg_ref[br:br + 1, :])
        v = z[:, base + LANES:base + 2 * LANES]
        kv_refs[br][:, 0:LANES] = k
        kv_refs[br][:, LANES:2 * LANES] = v
        if g_refs[br] is not None:
            g_refs[br][0] = jnp.where(low, k, pltpu.roll(v, HEAD_DIM, 1))
            g_refs[br][1] = jnp.where(low, pltpu.roll(k, HEAD_DIM, 1), v)
    gates_ref[...] = jax.nn.sigmoid(z[:, COL_GATE:COL_GATE + LANES])
    u_ref[...] = z[:, COL_U:COL_U + SSM_WIDTH]


def _proj(x2, mod, rows_per_mod, norm_g, w_hi, w_lo, qg, kg, bd, cos, sin_p, sin_m, tm):
    rows = x2.shape[0]
    nt = rows // tm
    rm = mod.shape[2]
    tab_tiles = cos.shape[0] // tm
    mod_tiles = rows_per_mod // tm
    row = lambda w: pl.BlockSpec((tm, w), lambda i: (i, 0))
    out_shapes = [jax.ShapeDtypeStruct((rows, ATTN_WIDTH), F32),
                  jax.ShapeDtypeStruct((rows, KV_COLS), F32),
                  jax.ShapeDtypeStruct((rows, KV_COLS), F32),
                  jax.ShapeDtypeStruct((rows, KV_COLS), F32),
                  jax.ShapeDtypeStruct((N_KV, rows, LANES), F32),
                  jax.ShapeDtypeStruct((N_KV, rows, LANES), F32),
                  jax.ShapeDtypeStruct((rows, LANES), F32),
                  jax.ShapeDtypeStruct((rows, SSM_WIDTH), F32)]
    out_specs = [row(ATTN_WIDTH), row(KV_COLS), row(KV_COLS), row(KV_COLS),
                 pl.BlockSpec((N_KV, tm, LANES), lambda i: (0, i, 0)),
                 pl.BlockSpec((N_KV, tm, LANES), lambda i: (0, i, 0)),
                 row(LANES), row(SSM_WIDTH)]
    tab = pl.BlockSpec((tm, LANES), lambda i: (i % tab_tiles, 0))
    return pl.pallas_call(
        _proj_kernel,
        grid=(nt,),
        in_specs=[row(D_MODEL),
                  pl.BlockSpec((1, 6, rm, D_MODEL), lambda i: (i // mod_tiles, 0, 0, 0)),
                  _vmem_full(), _vmem_full(), _vmem_full(), _vmem_full(), _vmem_full(), _vmem_full(),
                  tab, tab, tab],
        out_specs=out_specs,
        out_shape=out_shapes,
        compiler_params=_cparams("arbitrary"),
        name="proj",
    )(x2, mod, norm_g, w_hi, w_lo, qg, kg, bd, cos, sin_p, sin_m)


def _cmp1_kernel(*refs, nblk, rb, npre):
    k_refs = refs[npre:npre + nblk]
    v_refs = refs[npre + nblk:npre + 2 * nblk]
    wk_hi, wk_lo, wv_hi, wv_lo, abk_ref, abv_ref = refs[npre + 2 * nblk:]
    nch = rb // CMP_STRIDE

    def flat(row_refs):
        return jnp.concatenate(
            [jnp.concatenate([ref[0, pl.ds(r, nch, stride=CMP_STRIDE), :] for ref in row_refs], axis=0)
             for r in range(CMP_STRIDE)], axis=1)

    abk_ref[0] = _mm3(flat(k_refs), wk_hi[...], wk_lo[...])
    abv_ref[0] = _mm3(flat(v_refs), wv_hi[...], wv_lo[...])


def _cmp1_rows(rows3, w, tr):
    bn, length, _ = rows3.shape
    nch_t = tr // CMP_STRIDE
    n_chunk = length // CMP_STRIDE
    out = jax.ShapeDtypeStruct((bn, n_chunk, 4 * CMP_HID), F32)
    ospec = pl.BlockSpec((1, nch_t, 4 * CMP_HID), lambda b, t: (b, t, 0))
    return pl.pallas_call(
        functools.partial(_cmp1_kernel, nblk=1, rb=tr, npre=0),
        grid=(bn, length // tr),
        in_specs=[pl.BlockSpec((1, tr, LANES), lambda b, t: (b, t, 0)),
                  pl.BlockSpec((1, tr, LANES), lambda b, t: (b, t, 1)),
                  _vmem_full(), _vmem_full(), _vmem_full(), _vmem_full()],
        out_specs=[ospec, ospec],
        out_shape=[out, out],
        compiler_params=_cparams("arbitrary", "arbitrary"),
        name="cmp1_rows",
    )(rows3, rows3, *w)


def _cmp1_pages(page_table, cache3, w, npg):
    bn, n_pages = page_table.shape
    nch_t = npg * PAGE // CMP_STRIDE
    n_chunk = n_pages * PAGE // CMP_STRIDE
    out = jax.ShapeDtypeStruct((bn, n_chunk, 4 * CMP_HID), F32)
    ospec = pl.BlockSpec((1, nch_t, 4 * CMP_HID), lambda b, t, pt: (b, t, 0))
    page_specs = [pl.BlockSpec((1, PAGE, LANES),
                               functools.partial(lambda b, t, pt, k, kv: (pt[b, t * npg + k], 0, kv), k=k, kv=kv))
                  for kv in range(2) for k in range(npg)]
    return pl.pallas_call(
        functools.partial(_cmp1_kernel, nblk=npg, rb=PAGE, npre=1),
        grid_spec=pltpu.PrefetchScalarGridSpec(
            num_scalar_prefetch=1,
            grid=(bn, n_pages // npg),
            in_specs=page_specs + [_vmem_full()] * 4,
            out_specs=[ospec, ospec]),
        out_shape=[out, out],
        compiler_params=_cparams("arbitrary", "arbitrary"),
        name="cmp1_pages",
    )(page_table, *([cache3] * (2 * npg)), *w)


def _cmp2_kernel(abk_ref, abv_ref, pek_ref, pev_ref, w1k_ref, w1v_ref, w2k_ref, w2v_ref, kc_ref, vc_ref):
    nch = abk_ref.shape[1]
    for ab_ref, pe_ref, w1_ref, w2_ref, o_ref in ((abk_ref, pek_ref, w1k_ref, w2k_ref, kc_ref),
                                                  (abv_ref, pev_ref, w1v_ref, w2v_ref, vc_ref)):
        bias = _dotx(pe_ref[...], w1_ref[...])[0:1, :]
        for g in range(N_KV):
            a = ab_ref[0, :, g * 2 * CMP_HID:g * 2 * CMP_HID + CMP_HID]
            bm = ab_ref[0, :, g * 2 * CMP_HID + CMP_HID:(g + 1) * 2 * CMP_HID]
            hid = a + pltpu.roll(bm, nch - 1, 0) + bias
            o_ref[0, :, g * HEAD_DIM:(g + 1) * HEAD_DIM] = _dotx(jax.nn.gelu(hid), w2_ref[...])


def _cmp2(abk, abv, pek, pev, w1k, w1v, w2k, w2v):
    bn, nch, _ = abk.shape
    out = jax.ShapeDtypeStruct((bn, nch, N_KV * HEAD_DIM), F32)
    ab = pl.BlockSpec((1, nch, 4 * CMP_HID), lambda b: (b, 0, 0))
    ospec = pl.BlockSpec((1, nch, N_KV * HEAD_DIM), lambda b: (b, 0, 0))
    return pl.pallas_call(
        _cmp2_kernel,
        grid=(bn,),
        in_specs=[ab, ab] + [_vmem_full()] * 6,
        out_specs=[ospec, ospec],
        out_shape=[out, out],
        compiler_params=_cparams("arbitrary"),
        name="cmp2",
    )(abk, abv, pek, pev, w1k, w1v, w2k, w2v)


def _sel_kernel(qpos_ref, q_ref, kc_ref, vc_ref, ov_ref, oc_ref, sel_ref, *, ksel):
    q = q_ref[0]
    kc = kc_ref[0]
    vc = vc_ref[0]
    qpos = qpos_ref[0]
    tq = q.shape[0]
    ncp = kc.shape[0]
    nsp = ov_ref.shape[1]
    ci = lax.broadcasted_iota(jnp.int32, (tq, ncp), 1)
    cmask = (ci * CMP_STRIDE + (CMP_LEN - 1)) <= qpos
    ji = lax.broadcasted_iota(jnp.int32, (tq, nsp), 1)
    jf = ji.astype(F32)
    cur = qpos // SLC_BLOCK
    future = ji > cur
    forced = (ji == 0) | (ji == cur) | (ji == cur - 1)
    for g in range(N_KV):
        kg = kc[:, g * HEAD_DIM:(g + 1) * HEAD_DIM]
        vg = vc[:, g * HEAD_DIM:(g + 1) * HEAD_DIM]
        psum = jnp.zeros((tq, ncp), F32)
        for r in range(GQA):
            h = g * GQA + r
            s = _dotx_t(q[:, h * HEAD_DIM:(h + 1) * HEAD_DIM], kg)
            s = jnp.where(cmask, s, NEG)
            e = jnp.exp(s - jnp.max(s, axis=-1, keepdims=True))
            p = jnp.where(cmask, e / jnp.sum(e, axis=-1, keepdims=True), 0.0)
            psum = psum + p
            oc_ref[0, :, h * HEAD_DIM:(h + 1) * HEAD_DIM] = _dotx(p, vg)
        imp = _dotx(psum, ov_ref[...])
        imp = jnp.where(future, NEG, jnp.where(forced, BIG, imp))
        sel = jnp.zeros((tq, nsp), F32)
        for _ in range(ksel):
            m = jnp.max(imp, axis=-1, keepdims=True)
            idx = jnp.min(jnp.where(imp == m, jf, float(nsp)), axis=-1, keepdims=True)
            hit = jf == idx
            sel = jnp.where(hit, 1.0, sel)
            imp = jnp.where(hit, -jnp.inf, imp)
        sel_ref[0, g] = sel


def _select(qpos, q, kc, vc, ov, tq, ksel):
    bn, length, _ = q.shape
    ncp = kc.shape[1]
    nsp = ov.shape[1]
    return pl.pallas_call(
        functools.partial(_sel_kernel, ksel=ksel),
        grid=(bn, length // tq),
        in_specs=[pl.BlockSpec((1, tq, 1), lambda b, c: (b, c, 0)),
                  pl.BlockSpec((1, tq, ATTN_WIDTH), lambda b, c: (b, c, 0)),
                  pl.BlockSpec((1, ncp, N_KV * HEAD_DIM), lambda b, c: (b, 0, 0)),
                  pl.BlockSpec((1, ncp, N_KV * HEAD_DIM), lambda b, c: (b, 0, 0)),
                  _vmem_full()],
        out_specs=[pl.BlockSpec((1, tq, ATTN_WIDTH), lambda b, c: (b, c, 0)),
                   pl.BlockSpec((1, N_KV, tq, nsp), lambda b, c: (b, 0, c, 0))],
        out_shape=[jax.ShapeDtypeStruct((bn, length, ATTN_WIDTH), F32),
                   jax.ShapeDtypeStruct((bn, N_KV, length, nsp), F32)],
        compiler_params=_cparams("arbitrary", "arbitrary"),
        name="select",
    )(qpos, q, kc, vc, ov)


def _attend_prompt_kernel(q_ref, kvs_ref, kvw_ref, sel_ref, oc_ref, gates_ref, e_ref, gexp_ref, o_ref):
    c = pl.program_id(2)
    q2 = q_ref[0]
    rows = GQA * Q_CHUNK
    qs = jnp.concatenate([q2[:, r * HEAD_DIM:(r + 1) * HEAD_DIM] for r in range(GQA)], axis=0)
    q_hi, q_lo = _split(qs)
    qpos = c * Q_CHUNK + (lax.broadcasted_iota(jnp.int32, (rows, 1), 0) & (Q_CHUNK - 1))
    selb = sel_ref[0, 0].astype(BF16)

    def body(t, carry):
        m, l, acc = carry
        tile = kvs_ref[0, 0, pl.ds(pl.multiple_of(t * KV_TILE, KV_TILE), KV_TILE), :]
        s = _mm3_t(q_hi, q_lo, tile[:, 0:HEAD_DIM])
        selx = _dot(selb, e_ref[t])
        selx = jnp.concatenate([selx] * GQA, axis=0)
        kpos = t * KV_TILE + lax.broadcasted_iota(jnp.int32, (1, KV_TILE), 1)
        mask = (selx > 0.5) & (kpos <= qpos)
        s = jnp.where(mask, s, NEG)
        m_new = jnp.maximum(m, jnp.max(s, axis=-1, keepdims=True))
        alpha = jnp.exp(m - m_new)
        p = jnp.where(mask, jnp.exp(s - m_new), 0.0)
        l = alpha * l + jnp.sum(p, axis=-1, keepdims=True)
        acc = alpha * acc + _dot(p.astype(BF16), tile.astype(BF16))
        return m_new, l, acc

    nt = c // (KV_TILE // Q_CHUNK) + 1
    m, l, acc = lax.fori_loop(0, nt, body, (jnp.full((rows, 1), NEG, F32), jnp.zeros((rows, 1), F32),
                                             jnp.zeros((rows, LANES), F32)))
    o_s = acc / jnp.where(l > 0.0, l, 1.0)

    wlen = WINDOW + Q_CHUNK
    ws = pl.multiple_of(jnp.maximum(c * Q_CHUNK - WINDOW, 0), Q_CHUNK)
    wt = kvw_ref[0, 0, pl.ds(ws, wlen), :]
    s = _mm3_t(q_hi, q_lo, wt[:, 0:HEAD_DIM])
    kpos = ws + lax.broadcasted_iota(jnp.int32, (1, wlen), 1)
    mask = (kpos <= qpos) & (kpos >= qpos - WINDOW)
    s = jnp.where(mask, s, NEG)
    e = jnp.exp(s - jnp.max(s, axis=-1, keepdims=True))
    p = jnp.where(mask, e / jnp.sum(e, axis=-1, keepdims=True), 0.0)
    o_w = _dot(p.astype(BF16), wt.astype(BF16))

    lane = lax.broadcasted_iota(jnp.int32, (Q_CHUNK, LANES), 1)
    low = lane < HEAD_DIM

    def heads_to_lanes(o):
        pairs = []
        for hp in range(GQA // 2):
            even = o[(2 * hp) * Q_CHUNK:(2 * hp + 1) * Q_CHUNK]
            odd = o[(2 * hp + 1) * Q_CHUNK:(2 * hp + 2) * Q_CHUNK]
            pairs.append(jnp.where(low, pltpu.roll(even, HEAD_DIM, 1), odd))
        return jnp.concatenate(pairs, axis=1)

    gates = gates_ref[0]
    out = (_dotx(gates, gexp_ref[0]) * oc_ref[0]
           + _dotx(gates, gexp_ref[1]) * heads_to_lanes(o_s)
           + _dotx(gates, gexp_ref[2]) * heads_to_lanes(o_w))
    o_ref[0] = out


def _attend_prompt(q, kvsg, kvwg, sel, oc, gates, e_tab, gexp):
    bn, length, _ = q.shape
    gw = GQA * HEAD_DIM
    return pl.pallas_call(
        _attend_prompt_kernel,
        grid=(bn, N_KV, length // Q_CHUNK),
        in_specs=[pl.BlockSpec((1, Q_CHUNK, gw), lambda b, g, c: (b, c, g)),
                  pl.BlockSpec((1, 1, length, LANES), lambda b, g, c: (g, b, 0, 0)),
                  pl.BlockSpec((1, 1, length, LANES), lambda b, g, c: (g, b, 0, 0)),
                  pl.BlockSpec((1, 1, Q_CHUNK, sel.shape[3]), lambda b, g, c: (b, g, c, 0)),
                  pl.BlockSpec((1, Q_CHUNK, gw), lambda b, g, c: (b, c, g)),
                  pl.BlockSpec((1, Q_CHUNK, LANES), lambda b, g, c: (b, c, 0)),
                  _vmem_full(),
                  pl.BlockSpec((N_BRANCH, LANES, gw), lambda b, g, c: (0, 0, g))],
        out_specs=pl.BlockSpec((1, Q_CHUNK, gw), lambda b, g, c: (b, c, g)),
        out_shape=jax.ShapeDtypeStruct((bn, length, ATTN_WIDTH), F32),
        compiler_params=_cparams("arbitrary", "arbitrary", "arbitrary"),
        name="attend_prompt",
    )(q, kvsg, kvwg, sel, oc, gates, e_tab, gexp)


def _attend_sample_kernel(*refs, npg, n_rows):
    pt_ref = refs[0]
    page_refs = refs[1:1 + npg]
    (qb_ref, selc_ref, sell_ref, news_ref, win_ref, neww_ref, e_ref,
     os_ref, ow_ref, m_ref, l_ref, acc_ref) = refs[1 + npg:]
    del pt_ref
    j = pl.program_id(1)
    nq = n_rows // (N_KV * GQA)
    qb = qb_ref[0]
    q_hi, q_lo = _split(qb)
    rowi = lax.broadcasted_iota(jnp.int32, (n_rows, 1), 0)
    qi = rowi & (nq - 1)

    @pl.when(j == 0)
    def _():
        m_ref[...] = jnp.full(m_ref.shape, NEG, F32)
        l_ref[...] = jnp.zeros(l_ref.shape, F32)
        acc_ref[...] = jnp.zeros(acc_ref.shape, F32)

    rows = jnp.concatenate([ref[0] for ref in page_refs], axis=0)
    s = _mm3_t(q_hi, q_lo, rows)
    mask = _dot(selc_ref[0, 0].astype(BF16), e_ref[...]) > 0.5
    s = jnp.where(mask, s, NEG)
    m_old = m_ref[...]
    m_new = jnp.maximum(m_old, jnp.max(s, axis=-1, keepdims=True))
    alpha = jnp.exp(m_old - m_new)
    p = jnp.where(mask, jnp.exp(s - m_new), 0.0)
    l_ref[...] = alpha * l_ref[...] + jnp.sum(p, axis=-1, keepdims=True)
    acc_ref[...] = alpha * acc_ref[...] + _dot(p.astype(BF16), rows.astype(BF16))
    m_ref[...] = m_new

    @pl.when(j == pl.num_programs(1) - 1)
    def _():
        coli = lax.broadcasted_iota(jnp.int32, (1, SUBLANES), 1)
        causal = (coli <= qi) & (coli < nq)
        new = news_ref[0]
        s2 = _mm3_t(q_hi, q_lo, new)
        mask2 = causal & (sell_ref[0][:, 0:1] > 0.5)
        s2 = jnp.where(mask2, s2, NEG)
        m_old = m_ref[...]
        m_new = jnp.maximum(m_old, jnp.max(s2, axis=-1, keepdims=True))
        alpha = jnp.exp(m_old - m_new)
        p2 = jnp.where(mask2, jnp.exp(s2 - m_new), 0.0)
        l = alpha * l_ref[...] + jnp.sum(p2, axis=-1, keepdims=True)
        acc = alpha * acc_ref[...] + _dot(p2.astype(BF16), new.astype(BF16))
        os_ref[0] = acc / jnp.where(l > 0.0, l, 1.0)

        win = win_ref[0]
        nwin = win.shape[0]
        neww = neww_ref[0]
        s1 = _mm3_t(q_hi, q_lo, win)
        mask1 = lax.broadcasted_iota(jnp.int32, (1, nwin), 1) >= (qi + (nwin - WINDOW))
        s1 = jnp.where(mask1, s1, NEG)
        s3 = jnp.where(causal, _mm3_t(q_hi, q_lo, neww), NEG)
        mw = jnp.maximum(jnp.max(s1, axis=-1, keepdims=True), jnp.max(s3, axis=-1, keepdims=True))
        e1 = jnp.exp(s1 - mw)
        e3 = jnp.exp(s3 - mw)
        den = jnp.sum(e1, axis=-1, keepdims=True) + jnp.sum(e3, axis=-1, keepdims=True)
        p1 = jnp.where(mask1, e1 / den, 0.0)
        p3 = jnp.where(causal, e3 / den, 0.0)
        ow_ref[0] = _dot(p1.astype(BF16), win.astype(BF16)) + _dot(p3.astype(BF16), neww.astype(BF16))


def _attend_sample(page_table, cache3, qblk, selc, sell, news, win, neww, e_tab, npg):
    bn, n_pages = page_table.shape
    n_rows = qblk.shape[1]
    nwin = win.shape[1]
    steps = n_pages // npg
    page_specs = [pl.BlockSpec((1, PAGE, KV_COLS), functools.partial(lambda b, j, pt, k: (pt[b, j * npg + k], 0, 0), k=k))
                  for k in range(npg)]
    per_b = lambda r, w: pl.BlockSpec((1, r, w), lambda b, j, pt: (b, 0, 0))
    out = jax.ShapeDtypeStruct((bn, n_rows, KV_COLS), F32)
    return pl.pallas_call(
        functools.partial(_attend_sample_kernel, npg=npg, n_rows=n_rows),
        grid_spec=pltpu.PrefetchScalarGridSpec(
            num_scalar_prefetch=1,
            grid=(bn, steps),
            in_specs=page_specs + [
                per_b(n_rows, KV_COLS),
                pl.BlockSpec((1, 1, n_rows, LANES), lambda b, j, pt: (b, j, 0, 0)),
                per_b(n_rows, LANES),
                per_b(SUBLANES, KV_COLS),
                per_b(nwin, KV_COLS),
                per_b(SUBLANES, KV_COLS),
                pl.BlockSpec(memory_space=pltpu.VMEM)],
            out_specs=[per_b(n_rows, KV_COLS), per_b(n_rows, KV_COLS)],
            scratch_shapes=[pltpu.VMEM((n_rows, 1), F32), pltpu.VMEM((n_rows, 1), F32),
                            pltpu.VMEM((n_rows, KV_COLS), F32)]),
        out_shape=[out, out],
        compiler_params=_cparams("arbitrary", "arbitrary"),
        name="attend_sample",
    )(page_table, *([cache3] * npg), qblk, selc, sell, news, win, neww, e_tab)


def _combine_kernel(gates_ref, gexp_ref, oc_ref, os_ref, ow_ref, o_ref):
    gates = gates_ref[...]
    o_ref[...] = (_dotx(gates, gexp_ref[0]) * oc_ref[...] + _dotx(gates, gexp_ref[1]) * os_ref[...]
                  + _dotx(gates, gexp_ref[2]) * ow_ref[...])


def _combine(gates, gexp, oc, o_s, o_w):
    rows = gates.shape[0]
    return pl.pallas_call(
        _combine_kernel,
        out_shape=jax.ShapeDtypeStruct((rows, ATTN_WIDTH), F32),
        compiler_params=pltpu.CompilerParams(vmem_limit_bytes=VMEM_LIMIT),
        name="combine",
    )(gates, gexp, oc, o_s, o_w)


def _ssm_prep_kernel(are_ref, aim_ref, ldt_ref, bre_ref, bim_ref, lam_ref, bbre_ref, bbim_ref, pre_ref, pim_ref):
    a_re = are_ref[...]
    a_im = aim_ref[...]
    dt = jnp.exp(ldt_ref[...])
    mag = jnp.exp(a_re * dt)
    lb_re = mag * jnp.cos(a_im * dt)
    lb_im = mag * jnp.sin(a_im * dt)
    den = a_re * a_re + a_im * a_im
    n_re = lb_re - 1.0
    f_re = (n_re * a_re + lb_im * a_im) / den
    f_im = (lb_im * a_re - n_re * a_im) / den
    b_re = bre_ref[...]
    b_im = bim_ref[...]
    bbre_ref[...] = f_re[0:1] * b_re - f_im[0:1] * b_im
    bbim_ref[...] = f_re[0:1] * b_im + f_im[0:1] * b_re
    lam_ref[0] = lb_re
    lam_ref[1] = lb_im
    pre_ref[0:1, :] = lb_re[0:1]
    pim_ref[0:1, :] = lb_im[0:1]

    def body(i, carry):
        pr, pi = carry
        nr = pr * lb_re[0:1] - pi * lb_im[0:1]
        ni = pr * lb_im[0:1] + pi * lb_re[0:1]
        pre_ref[pl.ds(i, 1), :] = nr
        pim_ref[pl.ds(i, 1), :] = ni
        return nr, ni

    lax.fori_loop(1, SCAN_SEG, body, (lb_re[0:1], lb_im[0:1]))


def _ssm_prep(a_re, a_im, ldt, b_re, b_im):
    vec = jax.ShapeDtypeStruct((SSM_GROUP, N_STATE), F32)
    return pl.pallas_call(
        _ssm_prep_kernel,
        out_shape=[jax.ShapeDtypeStruct((2, SUBLANES, N_STATE), F32), vec, vec,
                   jax.ShapeDtypeStruct((SCAN_SEG, N_STATE), F32), jax.ShapeDtypeStruct((SCAN_SEG, N_STATE), F32)],
        compiler_params=pltpu.CompilerParams(vmem_limit_bytes=VMEM_LIMIT),
        name="ssm_prep",
    )(a_re, a_im, ldt, b_re, b_im)


SCAN_SLABS = N_STATE // LANES
SCAN_GROUP = 4


def _scan_prompt_kernel(u_ref, lam_ref, pre_ref, pim_ref, wb_ref, wcr_ref, wci_ref, d_ref,
                        y_ref, hre_out, him_out, hre, him, st):
    t = pl.program_id(1)
    tt = u_ref.shape[1]
    nseg = tt // SCAN_SEG

    @pl.when(t == 0)
    def _():
        st[...] = jnp.zeros(st.shape, F32)

    u = u_ref[0]
    gchunk = u.shape[1] // wb_ref.shape[0]
    half = wb_ref.shape[2] // 2
    spc = half // LANES
    for k in range(wb_ref.shape[0]):
        bu = _dotx(u[:, k * gchunk:(k + 1) * gchunk], wb_ref[k])
        for j in range(spc):
            hre[k * spc + j] = bu[:, j * LANES:(j + 1) * LANES]
            him[k * spc + j] = bu[:, half + j * LANES:half + (j + 1) * LANES]

    for s0 in range(0, SCAN_SLABS, SCAN_GROUP):
        lam = [(lam_ref[0, 0:1, (s0 + j) * LANES:(s0 + j + 1) * LANES],
                lam_ref[1, 0:1, (s0 + j) * LANES:(s0 + j + 1) * LANES]) for j in range(SCAN_GROUP)]

        def step(i, carry, s0=s0, lam=lam):
            out = []
            rows = pl.ds(i, nseg, stride=SCAN_SEG)
            for j in range(SCAN_GROUP):
                hr, hi = carry[j]
                lr, li = lam[j]
                nr = lr * hr - li * hi + hre[s0 + j, rows, :]
                ni = lr * hi + li * hr + him[s0 + j, rows, :]
                hre[s0 + j, rows, :] = nr
                him[s0 + j, rows, :] = ni
                out.append((nr, ni))
            return tuple(out)

        z = jnp.zeros((nseg, LANES), F32)
        lax.fori_loop(0, SCAN_SEG, step, tuple((z, z) for _ in range(SCAN_GROUP)), unroll=2)

    for c in range(SCAN_SLABS):
        cs = slice(c * LANES, (c + 1) * LANES)
        cr = st[0:1, cs]
        ci = st[1:2, cs]
        pr = pre_ref[:, cs]
        pi = pim_ref[:, cs]
        lsr = pr[SCAN_SEG - 1:SCAN_SEG]
        lsi = pi[SCAN_SEG - 1:SCAN_SEG]
        for k in range(nseg):
            rs = slice(k * SCAN_SEG, (k + 1) * SCAN_SEG)
            lr = hre[c, rs, :]
            li = him[c, rs, :]
            hre[c, rs, :] = lr + (pr * cr - pi * ci)
            him[c, rs, :] = li + (pr * ci + pi * cr)
            er = lr[SCAN_SEG - 1:SCAN_SEG]
            ei = li[SCAN_SEG - 1:SCAN_SEG]
            cr, ci = lsr * cr - lsi * ci + er, lsr * ci + lsi * cr + ei
        st[0:1, cs] = cr
        st[1:2, cs] = ci
        hre_out[0, :, cs] = jnp.broadcast_to(cr, (SUBLANES, LANES))
        him_out[0, :, cs] = jnp.broadcast_to(ci, (SUBLANES, LANES))

    sw = wcr_ref.shape[1]
    ow = wcr_ref.shape[2]
    spo = sw // LANES
    for k in range(wcr_ref.shape[0]):
        hr = jnp.concatenate([hre[k * spo + j] for j in range(spo)], axis=1)
        hi = jnp.concatenate([him[k * spo + j] for j in range(spo)], axis=1)
        yk = _mm3(hr, *_split(wcr_ref[k])) + _mm3(hi, *_split(wci_ref[k]))
        y_ref[0, :, k * ow:(k + 1) * ow] = yk + d_ref[:, k * ow:(k + 1) * ow] * u[:, k * ow:(k + 1) * ow]


def _scan_prompt(u, lam, p_re, p_im, wb, wcr, wci, dvec):
    bn, length, _ = u.shape
    tt = min(SCAN_TILE, length)
    state = jax.ShapeDtypeStruct((bn, SUBLANES, N_STATE), F32)
    sspec = pl.BlockSpec((1, SUBLANES, N_STATE), lambda b, t: (b, 0, 0))
    return pl.pallas_call(
        _scan_prompt_kernel,
        grid=(bn, length // tt),
        in_specs=[pl.BlockSpec((1, tt, SSM_WIDTH), lambda b, t: (b, t, 0))] + [_vmem_full()] * 7,
        out_specs=[pl.BlockSpec((1, tt, SSM_WIDTH), lambda b, t: (b, t, 0)), sspec, sspec],
        out_shape=[jax.ShapeDtypeStruct((bn, length, SSM_WIDTH), F32), state, state],
        scratch_shapes=[pltpu.VMEM((SCAN_SLABS, tt, LANES), F32), pltpu.VMEM((SCAN_SLABS, tt, LANES), F32),
                        pltpu.VMEM((SUBLANES, N_STATE), F32)],
        compiler_params=_cparams("arbitrary", "arbitrary"),
        name="scan_prompt",
    )(u, lam, p_re, p_im, wb, wcr, wci, dvec)


def _scan_sample_kernel(u_ref, h0r_ref, h0i_ref, lam_ref, wb_ref, wcr_ref, wci_ref, d_ref,
                        y_ref, hre_out, him_out, bre, bim, *, nb, nt):
    u = u_ref[...]
    gchunk = u.shape[1] // wb_ref.shape[0]
    half = wb_ref.shape[2] // 2
    for k in range(wb_ref.shape[0]):
        bu = _dotx(u[:, k * gchunk:(k + 1) * gchunk], wb_ref[k])
        bre[:, k * half:(k + 1) * half] = bu[:, 0:half]
        bim[:, k * half:(k + 1) * half] = bu[:, half:2 * half]
    lr = lam_ref[0, 0:1, :]
    li = lam_ref[1, 0:1, :]
    hr = h0r_ref[...]
    hi = h0i_ref[...]
    sw = wcr_ref.shape[1]
    ow = wcr_ref.shape[2]
    for t in range(nt):
        rows = slice(t * nb, (t + 1) * nb)
        hr, hi = lr * hr - li * hi + bre[rows, :], lr * hi + li * hr + bim[rows, :]
        ut = u[rows, :]
        for k in range(wcr_ref.shape[0]):
            yk = _dotx(hr[:, k * sw:(k + 1) * sw], wcr_ref[k]) + _dotx(hi[:, k * sw:(k + 1) * sw], wci_ref[k])
            y_ref[rows, k * ow:(k + 1) * ow] = yk + d_ref[:, k * ow:(k + 1) * ow] * ut[:, k * ow:(k + 1) * ow]
    hre_out[...] = hr
    him_out[...] = hi


def _scan_sample(u2, h0r, h0i, lam, wb, wcr, wci, dvec, nb, nt):
    rows = u2.shape[0]
    state = jax.ShapeDtypeStruct((nb, N_STATE), F32)
    return pl.pallas_call(
        functools.partial(_scan_sample_kernel, nb=nb, nt=nt),
        out_shape=[jax.ShapeDtypeStruct((rows, SSM_WIDTH), F32), state, state],
        scratch_shapes=[pltpu.VMEM((rows, N_STATE), F32), pltpu.VMEM((rows, N_STATE), F32)],
        compiler_params=pltpu.CompilerParams(vmem_limit_bytes=VMEM_LIMIT),
        name="scan_sample",
    )(u2, h0r, h0i, lam, wb, wcr, wci, dvec)


def _finish_kernel(x_ref, attn_ref, y_ref, mod_ref, wglu_ref, bglu_ref, ag_ref, sg_ref, woa_ref, wos_ref,
                   ng_ref, wg_ref, wu_ref, wd_ref, o_ref):
    def rms(v, g):
        return v * lax.rsqrt(jnp.mean(v * v, axis=-1, keepdims=True) + EPS) * g

    s = jax.nn.gelu(y_ref[...])
    s = s * jax.nn.sigmoid(_dot(s.astype(BF16), wglu_ref[...]) + bglu_ref[...])
    mix = (_dot(rms(attn_ref[...], ag_ref[...]).astype(BF16), woa_ref[...])
           + _dot(rms(s, sg_ref[...]).astype(BF16), wos_ref[...]))
    x = x_ref[...] + mod_ref[0, 2] * mix
    h = (rms(x, ng_ref[...]) * (1.0 + mod_ref[0, 4]) + mod_ref[0, 3]).astype(BF16)
    gate = _dot(h, wg_ref[...])
    up = _dot(h, wu_ref[...])
    f = _dot((gate * jax.nn.sigmoid(gate) * up).astype(BF16), wd_ref[...])
    o_ref[...] = x + mod_ref[0, 5] * f


def _finish(x2, attn2, y2, mod, rows_per_mod, weights, tm):
    rows = x2.shape[0]
    rm = mod.shape[2]
    mod_tiles = rows_per_mod // tm
    row = lambda w: pl.BlockSpec((tm, w), lambda i: (i, 0))
    return pl.pallas_call(
        _finish_kernel,
        grid=(rows // tm,),
        in_specs=[row(D_MODEL), row(ATTN_WIDTH), row(SSM_WIDTH),
                  pl.BlockSpec((1, 6, rm, D_MODEL), lambda i: (i // mod_tiles, 0, 0, 0))] + [_vmem_full()] * 10,
        out_specs=row(D_MODEL),
        out_shape=jax.ShapeDtypeStruct((rows, D_MODEL), F32),
        compiler_params=_cparams("arbitrary"),
        name="finish",
    )(x2, attn2, y2, mod, *weights)


def _overlap_table(ncp, ns, nsp):
    cs = np.arange(ncp) * CMP_STRIDE
    ss = np.arange(ns) * SLC_BLOCK
    ov = np.minimum(cs[:, None] + CMP_LEN, ss[None, :] + SLC_BLOCK) - np.maximum(cs[:, None], ss[None, :])
    out = np.zeros((ncp, nsp), np.float32)
    out[:, :ns] = np.clip(ov, 0, None) / CMP_LEN
    return jnp.asarray(out)


def _block_expand_table(n_tiles, n_blocks, tile_keys):
    j = np.arange(n_blocks)[None, :, None]
    k = np.arange(tile_keys)[None, None, :]
    t = np.arange(n_tiles)[:, None, None]
    return jnp.asarray((j == t * (tile_keys // SLC_BLOCK) + k // SLC_BLOCK), BF16)


def _gate_expand_table():
    out = np.zeros((N_BRANCH, LANES, ATTN_WIDTH), np.float32)
    for br in range(N_BRANCH):
        for h in range(N_HEADS):
            out[br, h * N_BRANCH + br, h * HEAD_DIM:(h + 1) * HEAD_DIM] = 1.0
    return jnp.asarray(out)


def _head_mean_table():
    out = np.zeros((LANES, LANES), np.float32)
    for h in range(LANES // HEAD_DIM):
        out[h * HEAD_DIM:(h + 1) * HEAD_DIM, h * HEAD_DIM:(h + 1) * HEAD_DIM] = 1.0 / HEAD_DIM
    return jnp.asarray(out)


def _rope_tables(pos):
    half = ROT_DIM // 2
    inv = ROPE_THETA ** (-(jnp.arange(half, dtype=F32) * 2.0 / ROT_DIM))
    ang = pos.astype(F32)[:, None] * inv[None, :]
    cos = jnp.cos(ang)
    sin = jnp.sin(ang)
    n = pos.shape[0]
    ones = jnp.ones((n, HEAD_DIM - ROT_DIM), F32)
    zeros = jnp.zeros((n, HEAD_DIM - ROT_DIM), F32)
    zh = jnp.zeros((n, half), F32)
    c = jnp.concatenate([cos, cos, ones], axis=1)
    sp = jnp.concatenate([zh, sin, zeros], axis=1)
    sm = jnp.concatenate([-sin, zh, zeros], axis=1)
    rep = LANES // HEAD_DIM
    return jnp.tile(c, (1, rep)), jnp.tile(sp, (1, rep)), jnp.tile(sm, (1, rep))


def _pack_w_in(w_in):
    o1 = ATTN_WIDTH
    o2 = o1 + N_BRANCH * KV_COLS
    o3 = o2 + N_BRANCH * N_HEADS
    pad = jnp.zeros((D_MODEL, IN_COLS_PACKED - COL_GATE - N_BRANCH * N_HEADS), F32)
    return jnp.concatenate([w_in[:, :o2], w_in[:, o3:], w_in[:, o2:o3], pad], axis=1)


def _cmp_weight(w1):
    wa = w1[:CMP_STRIDE * HEAD_DIM].reshape(CMP_STRIDE, HEAD_DIM, CMP_HID)
    wb = w1[CMP_STRIDE * HEAD_DIM:].reshape(CMP_STRIDE, HEAD_DIM, CMP_HID)
    wab = jnp.concatenate([wa, wb], axis=2)
    eye = jnp.eye(N_KV, dtype=F32)
    full = jnp.einsum('rdc,gh->rgdhc', wab, eye)
    return full.reshape(CMP_STRIDE * N_KV * HEAD_DIM, N_KV * 2 * CMP_HID)


def _ssm_weights(bb_re, bb_im, c_re, c_im):
    gpc = LANES // SSM_GROUP
    nchunk = N_SSM_GROUPS // gpc
    eye = jnp.eye(gpc, dtype=F32)

    def in_map(bb):
        b4 = bb.reshape(SSM_GROUP, nchunk, gpc, SSM_STATE)
        return jnp.einsum('jkgp,gh->kgjhp', b4, eye).reshape(nchunk, gpc * SSM_GROUP, gpc * SSM_STATE)

    wb = jnp.concatenate([in_map(bb_re), in_map(bb_im)], axis=2)

    def out_map(c):
        c4 = c.reshape(nchunk, gpc, SSM_GROUP, SSM_STATE)
        return jnp.einsum('kgip,gh->kgphi', c4, eye).reshape(nchunk, gpc * SSM_STATE, gpc * SSM_GROUP)

    return wb, out_map(c_re), -out_map(c_im)


def _shared_prep(prm):
    w_in = _pack_w_in(prm['w_in'])
    w_hi, w_lo = _split(w_in)
    qg = jnp.tile(prm['q_norm_g'][None, :], (1, LANES // HEAD_DIM))
    kg = jnp.tile(prm['k_norm_g'], (1, LANES // HEAD_DIM))
    cw = tuple(_split(_cmp_weight(prm['cmp_w1_k']))) + tuple(_split(_cmp_weight(prm['cmp_w1_v'])))
    pek = jnp.broadcast_to(prm['cmp_pe_k'].reshape(1, -1), (SUBLANES, CMP_LEN * HEAD_DIM))
    pev = jnp.broadcast_to(prm['cmp_pe_v'].reshape(1, -1), (SUBLANES, CMP_LEN * HEAD_DIM))
    flat = lambda a: jnp.broadcast_to(a.reshape(1, N_STATE), (SUBLANES, N_STATE))
    ldt = flat(jnp.repeat(prm['ssm_log_dt'], SSM_STATE))
    b_re = prm['ssm_b_re'].transpose(2, 0, 1).reshape(SSM_GROUP, N_STATE)
    b_im = prm['ssm_b_im'].transpose(2, 0, 1).reshape(SSM_GROUP, N_STATE)
    lam, bb_re, bb_im, p_re, p_im = _ssm_prep(flat(prm['ssm_a_re']), flat(prm['ssm_a_im']), ldt, b_re, b_im)
    wb, wcr, wci = _ssm_weights(bb_re, bb_im, prm['ssm_c_re'], prm['ssm_c_im'])
    dvec = prm['ssm_d'].reshape(1, SSM_WIDTH)
    fin = (prm['w_glu'].astype(BF16), prm['b_glu'].reshape(1, -1), prm['attn_out_g'].reshape(1, -1),
           prm['ssm_out_g'].reshape(1, -1), prm['w_out'][:ATTN_WIDTH].astype(BF16), prm['w_out'][ATTN_WIDTH:].astype(BF16),
           prm['norm_ffn_g'].reshape(1, -1), prm['w_ffn_gate'].astype(BF16), prm['w_ffn_up'].astype(BF16),
           prm['w_ffn_down'].astype(BF16))
    return dict(w_hi=w_hi, w_lo=w_lo, qg=qg, kg=kg, bd=_head_mean_table(), norm_g=prm['norm_mix_g'].reshape(1, -1),
                cw=cw, pek=pek, pev=pev, w1k=prm['cmp_w1_k'], w1v=prm['cmp_w1_v'], w2k=prm['cmp_w2_k'],
                w2v=prm['cmp_w2_v'], lam=lam, p_re=p_re, p_im=p_im, wb=wb, wcr=wcr, wci=wci, dvec=dvec, fin=fin,
                gexp=_gate_expand_table())


def _prompt_layer(x, ada, sp):
    bn, length, _ = x.shape
    rows = bn * length
    tm = min(256, length)
    pos = jnp.arange(length)
    cos, sin_p, sin_m = _rope_tables(pos)
    mod = ada.reshape(bn, 6, 1, D_MODEL)
    q, kvc, kvs, kvw, kvsg, kvwg, gates, u = _proj(
        x.reshape(rows, D_MODEL), mod, length, sp['norm_g'], sp['w_hi'], sp['w_lo'], sp['qg'], sp['kg'], sp['bd'],
        cos, sin_p, sin_m, tm)
    n_chunk = length // CMP_STRIDE
    abk, abv = _cmp1_rows(kvc.reshape(bn, length, KV_COLS), sp['cw'], min(2048, length))
    kc, vc = _cmp2(abk, abv, sp['pek'], sp['pev'], sp['w1k'], sp['w1v'], sp['w2k'], sp['w2v'])
    ns = length // SLC_BLOCK
    nsp = -(-ns // LANES) * LANES
    ov = _overlap_table(n_chunk, ns, nsp)
    qpos = jnp.broadcast_to(pos.astype(jnp.int32)[None, :, None], (bn, length, 1))
    q3 = q.reshape(bn, length, ATTN_WIDTH)
    oc, sel = _select(qpos, q3, kc, vc, ov, Q_CHUNK, min(TOP_K, ns))
    e_tab = _block_expand_table(length // KV_TILE, nsp, KV_TILE)
    attn = _attend_prompt(q3, kvsg.reshape(N_KV, bn, length, LANES), kvwg.reshape(N_KV, bn, length, LANES), sel, oc,
                          gates.reshape(bn, length, LANES), e_tab, sp['gexp'])
    y_ssm, h_re, h_im = _scan_prompt(u.reshape(bn, length, SSM_WIDTH), sp['lam'], sp['p_re'], sp['p_im'],
                                     sp['wb'], sp['wcr'], sp['wci'], sp['dvec'])
    y = _finish(x.reshape(rows, D_MODEL), attn.reshape(rows, ATTN_WIDTH), y_ssm.reshape(rows, SSM_WIDTH), mod, length,
                sp['fin'], tm)
    kv_shape = (bn, length, 2, N_KV, HEAD_DIM)
    keep = min(WINDOW, length)
    return (y.reshape(bn, length, D_MODEL), kvc.reshape(kv_shape), kvs.reshape(kv_shape),
            kvw.reshape(kv_shape)[:, length - keep:],
            h_re[:, 0].reshape(bn, N_SSM_GROUPS, SSM_STATE), h_im[:, 0].reshape(bn, N_SSM_GROUPS, SSM_STATE))


def _sample_layer(x, ada, cache_cmp, cache_slc, cache_win, st_re, st_im, page_table, sp):
    bn, length, _ = x.shape
    rows = bn * length
    n_pages = page_table.shape[1]
    past = n_pages * PAGE
    pos = past + jnp.arange(length)
    cos, sin_p, sin_m = _rope_tables(jnp.tile(pos, bn))
    mod = jnp.repeat(ada.reshape(bn, 6, D_MODEL), length, axis=0).reshape(rows, 6, D_MODEL).transpose(1, 0, 2)[None]
    q, kvc, kvs, kvw, _, _, gates, u = _proj(
        x.reshape(rows, D_MODEL), mod, rows, sp['norm_g'], sp['w_hi'], sp['w_lo'], sp['qg'], sp['kg'], sp['bd'],
        cos, sin_p, sin_m, rows)
    n_phys = cache_cmp.shape[0]
    npg = min(SLC_PAGES, n_pages)
    abk, abv = _cmp1_pages(page_table, cache_cmp.reshape(n_phys, PAGE, KV_COLS), sp['cw'], npg)
    kc, vc = _cmp2(abk, abv, sp['pek'], sp['pev'], sp['w1k'], sp['w1v'], sp['w2k'], sp['w2v'])
    n_chunk = (past + length) // CMP_STRIDE
    total = past + length
    ns = -(-total // SLC_BLOCK)
    nsp = -(-ns // LANES) * LANES
    ov = _overlap_table(n_chunk, ns, nsp)
    qpos = jnp.broadcast_to(pos.astype(jnp.int32)[None, :, None], (bn, length, 1))
    q3 = q.reshape(bn, length, ATTN_WIDTH)
    oc, sel = _select(qpos, q3, kc, vc, ov, length, min(TOP_K, ns))

    n_rows = N_KV * GQA * length
    q5 = q3.reshape(bn, length, N_KV, GQA, HEAD_DIM).transpose(0, 2, 3, 1, 4)
    eye = jnp.eye(N_KV, dtype=F32)
    qblk = jnp.einsum('bgrqd,gh->bgrqhd', q5, eye).reshape(bn, n_rows, N_KV * HEAD_DIM)
    qblk = jnp.concatenate([qblk, jnp.zeros_like(qblk)], axis=2)
    sel_rows = jnp.broadcast_to(sel[:, :, None], (bn, N_KV, GQA, length, nsp)).reshape(bn, n_rows, nsp)
    steps = n_pages // npg
    bps = npg * PAGE // SLC_BLOCK
    selc = sel_rows[:, :, :steps * bps].reshape(bn, n_rows, steps, bps).transpose(0, 2, 1, 3)
    selc = jnp.pad(selc, ((0, 0), (0, 0), (0, 0), (0, LANES - bps)))
    sell = jnp.pad(sel_rows[:, :, steps * bps:steps * bps + 1], ((0, 0), (0, 0), (0, LANES - 1)))
    pad_rows = lambda a: jnp.pad(a.reshape(bn, length, KV_COLS), ((0, 0), (0, SUBLANES - length), (0, 0)))
    e_tab = _block_expand_table(1, LANES, npg * PAGE)[0]
    o_s, o_w = _attend_sample(page_table, cache_slc.reshape(n_phys, PAGE, KV_COLS), qblk, selc, sell, pad_rows(kvs),
                              cache_win.reshape(bn, -1, KV_COLS), pad_rows(kvw), e_tab, npg)

    def rows_to_heads(o):
        o6 = o.reshape(bn, N_KV, GQA, length, 2 * N_KV, HEAD_DIM)
        pick = jnp.stack([o6[:, g, :, :, N_KV + g] for g in range(N_KV)], axis=1)
        return pick.transpose(0, 3, 1, 2, 4).reshape(rows, ATTN_WIDTH)

    attn = _combine(gates, sp['gexp'], oc.reshape(rows, ATTN_WIDTH), rows_to_heads(o_s), rows_to_heads(o_w))
    u_tm = u.reshape(bn, length, SSM_WIDTH).transpose(1, 0, 2).reshape(rows, SSM_WIDTH)
    y_tm, h_re, h_im = _scan_sample(u_tm, st_re.reshape(bn, N_STATE), st_im.reshape(bn, N_STATE), sp['lam'], sp['wb'],
                                    sp['wcr'], sp['wci'], sp['dvec'], bn, length)
    y_ssm = y_tm.reshape(length, bn, SSM_WIDTH).transpose(1, 0, 2).reshape(rows, SSM_WIDTH)
    y = _finish(x.reshape(rows, D_MODEL), attn, y_ssm, mod, rows, sp['fin'], rows)
    kv_shape = (bn, length, 2, N_KV, HEAD_DIM)
    return (y.reshape(bn, length, D_MODEL), kvc.reshape(kv_shape), kvs.reshape(kv_shape), kvw.reshape(kv_shape),
            h_re.reshape(bn, N_SSM_GROUPS, SSM_STATE), h_im.reshape(bn, N_SSM_GROUPS, SSM_STATE))


def kernel(x_prompt, x_sample, cache_kv_cmp, cache_kv_slc, cache_kv_win, state_ssm_re, state_ssm_im, page_table, c_prompt, c_sample, norm_mix_g, w_ada, b_ada, w_in, q_norm_g, k_norm_g, cmp_pe_k, cmp_w1_k, cmp_w2_k, cmp_pe_v, cmp_w1_v, cmp_w2_v, ssm_a_re, ssm_a_im, ssm_log_dt, ssm_b_re, ssm_b_im, ssm_c_re, ssm_c_im, ssm_d, w_glu, b_glu, attn_out_g, ssm_out_g, w_out, norm_ffn_g, w_ffn_gate, w_ffn_up, w_ffn_down):
    depth = norm_mix_g.shape[0]
    bp = x_prompt.shape[0]
    bs = x_sample.shape[0]
    xp, xs = x_prompt, x_sample
    outs = [[] for _ in range(10)]
    c_all = jnp.concatenate([c_prompt, c_sample], axis=0)
    c_all = jnp.pad(c_all, ((0, -(bp + bs) % SUBLANES), (0, 0)))
    for l in range(depth):
        prm = dict(
            norm_mix_g=norm_mix_g[l], w_in=w_in[l], q_norm_g=q_norm_g[l], k_norm_g=k_norm_g[l],
            cmp_pe_k=cmp_pe_k[l], cmp_w1_k=cmp_w1_k[l], cmp_w2_k=cmp_w2_k[l],
            cmp_pe_v=cmp_pe_v[l], cmp_w1_v=cmp_w1_v[l], cmp_w2_v=cmp_w2_v[l],
            ssm_a_re=ssm_a_re[l], ssm_a_im=ssm_a_im[l], ssm_log_dt=ssm_log_dt[l],
            ssm_b_re=ssm_b_re[l], ssm_b_im=ssm_b_im[l], ssm_c_re=ssm_c_re[l], ssm_c_im=ssm_c_im[l],
            ssm_d=ssm_d[l], w_glu=w_glu[l], b_glu=b_glu[l], attn_out_g=attn_out_g[l],
            ssm_out_g=ssm_out_g[l], w_out=w_out[l], norm_ffn_g=norm_ffn_g[l],
            w_ffn_gate=w_ffn_gate[l], w_ffn_up=w_ffn_up[l], w_ffn_down=w_ffn_down[l])
        sp = _shared_prep(prm)
        ada = _ada(c_all, w_ada[l], b_ada[l].reshape(1, -1))
        res_p = _prompt_layer(xp, ada[:bp], sp)
        res_s = _sample_layer(xs, ada[bp:bp + bs], cache_kv_cmp[l], cache_kv_slc[l], cache_kv_win[l],
                              state_ssm_re[l], state_ssm_im[l], page_table, sp)
        xp, xs = res_p[0], res_s[0]
        for i, a in enumerate(res_p[1:] + res_s[1:]):
            outs[i].append(a)
    n = [jnp.stack(o) for o in outs]
    return (xp, xs, n[0], n[1], n[2], n[3], n[4], n[5], n[6], n[7], n[8], n[9])
```
